```python
import math, functools
import jax, jax.numpy as jnp
from jax import lax
import numpy as np

D_MODEL = 1024
BATCH = 1
SEQ = 16384
DEPTH = 1
DEC_BATCH = 32
DEC_SEQ = 1
PAST_LEN = 16384
PAGE_SIZE = 128

N_META = 16
A_HEADS = D_MODEL // 256
A_QK = 64
A_V = 2 * A_QK
B_HEADS = D_MODEL // 128
B_HD = 64
IDX_HEADS = 8
IDX_DIM = 64
TOPK_MAX = 256
D_FF = ((8 * D_MODEL // 3 + 127) // 128) * 128
CONV_W = 3
ROPE_THETA = 500000.0
ROT_DIV = 4
Q_BLOCK = 128
EPS = 1e-6

COL_SIZES = (A_HEADS * 2 * A_QK, A_HEADS * 2 * A_QK, A_HEADS * A_V,
             B_HEADS * B_HD, B_HEADS * B_HD, B_HEADS * B_HD,
             IDX_HEADS * IDX_DIM, IDX_DIM, IDX_HEADS)
IN_COLS = 3 * A_HEADS * 2 * A_QK + 3 * B_HEADS * B_HD + IDX_HEADS * IDX_DIM + IDX_DIM + IDX_HEADS

kernel_name = "hybrid_diffattn_dsa_convffn_step"


def rmsnorm(x, g):
    xf = x.astype(jnp.float32)
    y = xf * lax.rsqrt(jnp.mean(xf * xf, axis=-1, keepdims=True) + EPS)
    return (y * g.astype(jnp.float32)).astype(x.dtype)


def rope(x, pos):
    rot = x.shape[-1] // ROT_DIV
    half = rot // 2
    inv = jnp.float32(ROPE_THETA) ** (-jnp.arange(0, rot, 2, dtype=jnp.float32) / rot)
    ang = pos.astype(jnp.float32)[:, None] * inv[None, :]
    shape = (ang.shape[0],) + (1,) * (x.ndim - 3) + (half,)
    cos = jnp.cos(ang).reshape(shape)
    sin = jnp.sin(ang).reshape(shape)
    xr = x[..., :rot].astype(jnp.float32)
    x1, x2 = xr[..., :half], xr[..., half:]
    rotated = jnp.concatenate([x1 * cos - x2 * sin, x2 * cos + x1 * sin], axis=-1)
    return jnp.concatenate([rotated.astype(x.dtype), x[..., rot:]], axis=-1)


def project(xn, pos, w_in):
    p = xn @ w_in
    offs = np.cumsum(COL_SIZES)[:-1].tolist()
    pq_a, pk_a, pv_a, pq_b, pk_b, pv_b, pq_i, pk_i, pw_i = jnp.split(p, offs, axis=-1)
    bn, tn = xn.shape[:2]
    qa = rope(pq_a.reshape(bn, tn, A_HEADS, 2, A_QK), pos)
    ka = rope(pk_a.reshape(bn, tn, A_HEADS, 2, A_QK), pos).reshape(bn, tn, A_HEADS, 2 * A_QK)
    va = pv_a.reshape(bn, tn, A_HEADS, A_V)
    qb = rope(pq_b.reshape(bn, tn, B_HEADS, B_HD), pos)
    kb = rope(pk_b.reshape(bn, tn, B_HEADS, B_HD), pos)
    vb = pv_b.reshape(bn, tn, B_HEADS, B_HD)
    qi = rope(pq_i.reshape(bn, tn, IDX_HEADS, IDX_DIM), pos)
    ki = rope(pk_i, pos)
    return (qa, ka, va, qb, kb, vb, qi, ki, pw_i)


def diff_attn(qa, ka, va, qpos, lam, lam_init, subln_g):
    bn, lk = ka.shape[0], ka.shape[1]
    k = ka.reshape(bn, lk, A_HEADS, 2, A_QK)
    s = jnp.einsum('bqhmd,bkhmd->bmhqk', qa, k).astype(jnp.float32) * (A_QK ** -0.5)
    mask = jnp.arange(lk)[None, :] <= qpos[:, None]
    p = jax.nn.softmax(jnp.where(mask, s, -jnp.inf), axis=-1)
    attn = p[:, 0] - lam * p[:, 1]
    o = jnp.einsum('bhqk,bkhe->bqhe', attn.astype(va.dtype), va)
    return rmsnorm(o, subln_g) * (1.0 - lam_init)


def dsa_attn(qb, qi, wi, ki_all, qpos, topk, fetch):
    lk = ki_all.shape[1]
    rel = jax.nn.relu(jnp.einsum('bqhd,bkd->bqhk', qi, ki_all).astype(jnp.float32) * (IDX_DIM ** -0.5))
    iscore = jnp.einsum('bqhk,bqh->bqk', rel, wi.astype(jnp.float32) * (IDX_HEADS ** -0.5))
    mask = jnp.arange(lk)[None, :] <= qpos[:, None]
    _, sel = lax.top_k(jnp.where(mask, iscore, -jnp.inf), topk)
    valid = sel <= qpos[None, :, None]
    ks, vs = fetch(sel)
    s = jnp.einsum('bqhd,bqjhd->bhqj', qb, ks).astype(jnp.float32) * (B_HD ** -0.5)
    p = jax.nn.softmax(jnp.where(valid[:, None], s, -jnp.inf), axis=-1)
    return jnp.einsum('bhqj,bqjhd->bqhd', p.astype(vs.dtype), vs)


def _take(rows, idx):
    return rows[idx]


def attend_prompt(proj, qpos, lam, lam_init, subln_g):
    qa, ka, va, qb, kb, vb, qi, ki, wi = proj
    bn, ln = qa.shape[:2]
    nb = -(-ln // Q_BLOCK)
    lp = nb * Q_BLOCK

    def pad(t):
        return jnp.pad(t, [(0, 0), (0, lp - ln)] + [(0, 0)] * (t.ndim - 2))

    def to_blocks(t):
        return jnp.moveaxis(pad(t).reshape((bn, nb, Q_BLOCK) + t.shape[2:]), 1, 0)

    def from_blocks(t):
        return jnp.moveaxis(t, 0, 1).reshape((bn, lp) + t.shape[3:])[:, :ln]

    kap, vap, kbp, vbp, kip = pad(ka), pad(va), pad(kb), pad(vb), pad(ki)
    qpos_pad = jnp.concatenate([qpos, qpos[-1] + 1 + jnp.arange(lp - ln, dtype=qpos.dtype)])
    topk = min(TOPK_MAX, ln // 4)
    take = jax.vmap(_take)

    def fetch(sel):
        return take(kbp, sel), take(vbp, sel)

    def one_block(args):
        qa_b, qb_b, qi_b, wi_b, qp = args
        oa = diff_attn(qa_b, kap, vap, qp, lam, lam_init, subln_g)
        ob = dsa_attn(qb_b, qi_b, wi_b, kip, qp, topk, fetch)
        return oa, ob

    oa, ob = lax.map(one_block, (to_blocks(qa), to_blocks(qb), to_blocks(qi), to_blocks(wi),
                                 qpos_pad.reshape(nb, Q_BLOCK)))
    return from_blocks(oa), from_blocks(ob)


def attend_sample(proj, qpos, page_table, pool_a_k, pool_a_v, pool_b_k, pool_b_v, pool_b_kidx,
                  lam, lam_init, subln_g):
    qa, ka, va, qb, kb, vb, qi, ki, wi = proj
    page = pool_a_k.shape[1]
    past_len = page_table.shape[1] * page

    def pages(pool):
        g = pool[page_table]
        return g.reshape((g.shape[0], past_len) + pool.shape[2:])

    ka_all = jnp.concatenate([pages(pool_a_k).astype(ka.dtype), ka], axis=1)
    va_all = jnp.concatenate([pages(pool_a_v).astype(va.dtype), va], axis=1)
    oa = diff_attn(qa, ka_all, va_all, qpos, lam, lam_init, subln_g)

    ki_all = jnp.concatenate([pages(pool_b_kidx).astype(ki.dtype), ki], axis=1)
    topk = min(TOPK_MAX, ki_all.shape[1] // 4)
    take = jax.vmap(_take)

    def fetch(sel):
        is_past = sel < past_len
        sp = jnp.where(is_past, sel, 0)
        sn = jnp.where(is_past, 0, sel - past_len)
        phys = take(page_table, sp // page)
        off = sp % page
        m = is_past[..., None, None]
        ks = jnp.where(m, pool_b_k[phys, off].astype(kb.dtype), take(kb, sn))
        vs = jnp.where(m, pool_b_v[phys, off].astype(vb.dtype), take(vb, sn))
        return ks, vs

    ob = dsa_attn(qb, qi, wi, ki_all, qpos, topk, fetch)
    return oa, ob


def conv_ffn(xn, prefix, w_up, conv_w, conv_b, w_down):
    u = xn @ w_up
    ue = jnp.concatenate([prefix.astype(u.dtype), u], axis=1)
    tn = u.shape[1]
    c = conv_b + ue[:, 0:tn] * conv_w[0]
    for j in range(1, CONV_W):
        c = c + ue[:, j:j + tn] * conv_w[j]
    gate, val = jnp.split(c, 2, axis=-1)
    h = jax.nn.gelu(gate, approximate=True) * val
    return h @ w_down, ue[:, -(CONV_W - 1):]


def layer(h, pos, attend, conv_prefix, ln_pre_mix, ln_post_mix, ln_pre_ffn, ln_post_ffn,
          w_in, w_gates, w_a_out, w_b_out, w_o, w_up, conv_w, conv_b, w_down):
    bn, tn = h.shape[:2]
    xn = rmsnorm(h, ln_pre_mix)
    proj = project(xn, pos, w_in)
    oa, ob = attend(proj, pos)
    ya = oa.reshape(bn, tn, -1) @ w_a_out
    yb = ob.reshape(bn, tn, -1) @ w_b_out
    ga, gb = jnp.split(jax.nn.sigmoid(xn @ w_gates), 2, axis=-1)
    mix = (ga * ya + gb * yb) @ w_o
    h = h + rmsnorm(mix, ln_post_mix)
    f, conv_state = conv_ffn(rmsnorm(h, ln_pre_ffn), conv_prefix, w_up, conv_w, conv_b, w_down)
    h = h + rmsnorm(f, ln_post_ffn)
    _, ka, va, _, kb, vb, _, ki, _ = proj
    return h, (ka, va, kb, vb, ki, conv_state)


def setup_inputs(seed: int = 0) -> dict:
    key = jax.random.key(seed)
    ks = jax.random.split(key, 26)
    f32 = jnp.float32

    def nrm(k, shape, scale=1.0):
        return jax.random.normal(k, shape, f32) * scale

    d = D_MODEL
    n_pages = PAST_LEN // PAGE_SIZE
    n_used = DEC_BATCH * n_pages
    n_pool = n_used + max(1, n_used // 4)
    page_table = jax.random.permutation(ks[7], n_pool)[:n_used].reshape(DEC_BATCH, n_pages).astype(jnp.int32)
    return {
        'x_prompt': nrm(ks[0], (BATCH, SEQ, d)),
        'x_sample': nrm(ks[1], (DEC_BATCH, DEC_SEQ, d)),
        'cache_a_k': nrm(ks[2], (DEPTH, n_pool, PAGE_SIZE, A_HEADS, 2 * A_QK)),
        'cache_a_v': nrm(ks[3], (DEPTH, n_pool, PAGE_SIZE, A_HEADS, A_V)),
        'cache_b_k': nrm(ks[4], (DEPTH, n_pool, PAGE_SIZE, B_HEADS, B_HD)),
        'cache_b_v': nrm(ks[5], (DEPTH, n_pool, PAGE_SIZE, B_HEADS, B_HD)),
        'cache_b_kidx': nrm(ks[6], (DEPTH, n_pool, PAGE_SIZE, IDX_DIM)),
        'state_ffn_conv': nrm(ks[8], (DEPTH, DEC_BATCH, CONV_W - 1, 2 * D_FF)),
        'page_table': page_table,
        'meta_tokens': nrm(ks[9], (N_META, d)),
        'ln_pre_mix': 1.0 + nrm(ks[10], (DEPTH, d), 0.05),
        'ln_post_mix': 1.0 + nrm(ks[11], (DEPTH, d), 0.05),
        'ln_pre_ffn': 1.0 + nrm(ks[12], (DEPTH, d), 0.05),
        'ln_post_ffn': 1.0 + nrm(ks[13], (DEPTH, d), 0.05),
        'w_in': nrm(ks[14], (DEPTH, d, IN_COLS), d ** -0.5),
        'lambda_q': nrm(ks[15], (DEPTH, 2, A_QK), 0.1),
        'lambda_k': nrm(ks[16], (DEPTH, 2, A_QK), 0.1),
        'subln_gain': 1.0 + nrm(ks[17], (DEPTH, A_V), 0.05),
        'w_gates': nrm(ks[18], (DEPTH, d, 2 * d), d ** -0.5),
        'w_a_out': nrm(ks[19], (DEPTH, A_HEADS * A_V, d), (A_HEADS * A_V) ** -0.5),
        'w_b_out': nrm(ks[20], (DEPTH, B_HEADS * B_HD, d), (B_HEADS * B_HD) ** -0.5),
        'w_o': nrm(ks[21], (DEPTH, d, d), d ** -0.5),
        'w_up': nrm(ks[22], (DEPTH, d, 2 * D_FF), d ** -0.5),
        'conv_w': nrm(ks[23], (DEPTH, CONV_W, 2 * D_FF), CONV_W ** -0.5),
        'conv_b': nrm(ks[24], (DEPTH, 2 * D_FF), 0.02),
        'w_down': nrm(ks[25], (DEPTH, D_FF, d), D_FF ** -0.5),
    }


def reference(x_prompt, x_sample, cache_a_k, cache_a_v, cache_b_k, cache_b_v, cache_b_kidx,
              state_ffn_conv, page_table, meta_tokens, ln_pre_mix, ln_post_mix, ln_pre_ffn,
              ln_post_ffn, w_in, lambda_q, lambda_k, subln_gain, w_gates, w_a_out, w_b_out, w_o,
              w_up, conv_w, conv_b, w_down):
    bp = x_prompt.shape[0]
    meta = jnp.broadcast_to(meta_tokens.astype(x_prompt.dtype)[None], (bp,) + meta_tokens.shape)
    h_p = jnp.concatenate([meta, x_prompt], axis=1)
    pos_p = jnp.arange(h_p.shape[1], dtype=jnp.int32)
    h_s = x_sample
    past_len = page_table.shape[1] * cache_a_k.shape[2]
    pos_s = past_len + jnp.arange(x_sample.shape[1], dtype=jnp.int32)

    new_p, new_s = [], []
    for l in range(DEPTH):
        lam_init = 0.8 - 0.6 * math.exp(-0.3 * l)
        lq = lambda_q[l].astype(jnp.float32)
        lk = lambda_k[l].astype(jnp.float32)
        lam = jnp.exp(jnp.sum(lq[0] * lk[0])) - jnp.exp(jnp.sum(lq[1] * lk[1])) + lam_init
        weights = (ln_pre_mix[l], ln_post_mix[l], ln_pre_ffn[l], ln_post_ffn[l], w_in[l], w_gates[l],
                   w_a_out[l], w_b_out[l], w_o[l], w_up[l], conv_w[l], conv_b[l], w_down[l])
        att_p = functools.partial(attend_prompt, lam=lam, lam_init=lam_init, subln_g=subln_gain[l])
        prefix_p = jnp.zeros((bp, CONV_W - 1, w_up.shape[-1]), h_p.dtype)
        h_p, st_p = layer(h_p, pos_p, att_p, prefix_p, *weights)
        att_s = functools.partial(attend_sample, page_table=page_table, pool_a_k=cache_a_k[l],
                                  pool_a_v=cache_a_v[l], pool_b_k=cache_b_k[l], pool_b_v=cache_b_v[l],
                                  pool_b_kidx=cache_b_kidx[l], lam=lam, lam_init=lam_init,
                                  subln_g=subln_gain[l])
        h_s, st_s = layer(h_s, pos_s, att_s, state_ffn_conv[l], *weights)
        new_p.append(st_p)
        new_s.append(st_s)

    y_prompt = h_p[:, N_META:]
    y_sample = h_s
    ak_p = jnp.stack([s[0] for s in new_p])
    av_p = jnp.stack([s[1] for s in new_p])
    bk_p = jnp.stack([s[2] for s in new_p])
    bv_p = jnp.stack([s[3] for s in new_p])
    bi_p = jnp.stack([s[4] for s in new_p])
    cv_p = jnp.stack([s[5] for s in new_p])
    ak_s = jnp.stack([s[0] for s in new_s])
    av_s = jnp.stack([s[1] for s in new_s])
    bk_s = jnp.stack([s[2] for s in new_s])
    bv_s = jnp.stack([s[3] for s in new_s])
    bi_s = jnp.stack([s[4] for s in new_s])
    cv_s = jnp.stack([s[5] for s in new_s])
    return (y_prompt, y_sample, ak_p, av_p, bk_p, bv_p, bi_p, cv_p, ak_s, av_s, bk_s, bv_s, bi_s, cv_s)
```

```python
import functools
import math

import numpy as np
import jax
import jax.numpy as jnp
from jax import lax
from jax.experimental import pallas as pl
from jax.experimental.pallas import tpu as pltpu

F32 = jnp.float32
BF16 = jnp.bfloat16
I32 = jnp.int32

D_MODEL = 1024
N_META = 16
A_HEADS = 4
A_QK = 64
A_V = 128
B_HEADS = 8
B_HD = 64
IDX_HEADS = 8
IDX_DIM = 64
TOPK_MAX = 256
D_FF = 2816
CONV_W = 3
ROPE_THETA = 500000.0
EPS = 1e-6

LANES = 128
HALF = 64
GROUPS = 4
ATT_COLS = GROUPS * LANES
IN_COLS_PAD = 29 * LANES
KIW_COL = 28 * LANES
NEG = -1e30
INT_MIN = -2 ** 31
INT_MAX = 2 ** 31 - 1

TQ = 256
TK = 512
TM = 256
FF_CHUNK = 256
DEC_PAGES = 8
IDX_PAGES = 16
VMEM_LIMIT = 56 * 1024 * 1024


def _cparams(sem):
    return pltpu.CompilerParams(dimension_semantics=sem, vmem_limit_bytes=VMEM_LIMIT)


def _rms(x, g):
    return x * lax.rsqrt(jnp.mean(x * x, axis=-1, keepdims=True) + EPS) * g


def _dot_t(a, b):
    return lax.dot_general(a, b, (((1,), (1,)), ((), ())), preferred_element_type=F32)


def _sort_key(x, valid):
    b = pltpu.bitcast(x, I32)
    key = jnp.where(b < 0, b ^ INT_MAX, b)
    key = jnp.where(x == 0.0, 0, key)
    return jnp.where(valid, key, INT_MIN)


def _proj_kernel(h_ref, g_ref, w_ref, tab_ref,
                 ka_ref, va_ref, kb_ref, vb_ref, kiw_ref,
                 qa16_ref, ka16_ref, va16_ref, qb16_ref, kb16_ref, vb16_ref, qi16_ref, kk16_ref):
    xn = _rms(h_ref[...], g_ref[...])
    p = jnp.dot(xn.astype(BF16), w_ref[...], preferred_element_type=F32)

    def rope(x, t0):
        c = tab_ref[:, t0 * LANES:(t0 + 1) * LANES]
        s1 = tab_ref[:, (t0 + 1) * LANES:(t0 + 2) * LANES]
        s2 = tab_ref[:, (t0 + 2) * LANES:(t0 + 3) * LANES]
        return x * c + pltpu.roll(x, LANES - 8, axis=1) * s1 + pltpu.roll(x, 8, axis=1) * s2

    def grp(base, g):
        return p[:, base + g * LANES: base + (g + 1) * LANES]

    for g in range(GROUPS):
        sl = slice(g * LANES, (g + 1) * LANES)
        qa16_ref[:, sl] = (rope(grp(0, g), 0) * (A_QK ** -0.5)).astype(BF16)
        ka = rope(grp(512, g), 0)
        ka_ref[:, sl] = ka
        ka16_ref[:, sl] = ka.astype(BF16)
        va = grp(1024, g)
        va_ref[:, sl] = va
        va16_ref[:, sl] = va.astype(BF16)
        qb16_ref[:, sl] = (rope(grp(1536, g), 0) * (B_HD ** -0.5)).astype(BF16)
        kb = rope(grp(2048, g), 0)
        kb_ref[:, sl] = kb
        kb16_ref[:, sl] = kb.astype(BF16)
        vb = grp(2560, g)
        vb_ref[:, sl] = vb
        vb16_ref[:, sl] = vb.astype(BF16)
        qi16_ref[:, sl] = (rope(grp(3072, g), 0) * (IDX_DIM ** -0.5)).astype(BF16)
    kiw = rope(p[:, KIW_COL:KIW_COL + LANES], 3)
    kiw_ref[...] = kiw
    lane = lax.broadcasted_iota(I32, kiw.shape, 1)
    kk16_ref[...] = jnp.where(lane < HALF, kiw, pltpu.roll(kiw, HALF, axis=1)).astype(BF16)


def _proj(h, g, w16, tab, tm):
    n = h.shape[0]
    row = lambda w: pl.BlockSpec((tm, w), lambda i: (i, 0))
    full = lambda a: pl.BlockSpec(a.shape, lambda i: (0, 0))
    f32o = lambda w: jax.ShapeDtypeStruct((n, w), F32)
    b16o = lambda w: jax.ShapeDtypeStruct((n, w), BF16)
    return pl.pallas_call(
        _proj_kernel,
        grid=(n // tm,),
        in_specs=[row(D_MODEL), full(g), full(w16), row(6 * LANES)],
        out_specs=[row(ATT_COLS)] * 4 + [row(LANES)] + [row(ATT_COLS)] * 7 + [row(LANES)],
        out_shape=[f32o(ATT_COLS)] * 4 + [f32o(LANES)] + [b16o(ATT_COLS)] * 7 + [b16o(LANES)],
        compiler_params=_cparams(("parallel",)),
        name="proj",
    )(h, g, w16, tab)


def _split_halves(x):
    lane = lax.broadcasted_iota(I32, x.shape, 1)
    zero = jnp.zeros_like(x)
    return jnp.concatenate([jnp.where(lane < HALF, x, zero), jnp.where(lane >= HALF, x, zero)], axis=0)


def _flash_kernel(qidx_ref, kidx_ref, q_ref, k_ref, v_ref, *rest, diff, lam_init):
    if diff:
        lq_ref, lk_ref, sg_ref, o_ref, qst_ref, m_ref, l_ref, acc_ref = rest
    else:
        bias_ref, o_ref, qst_ref, m_ref, l_ref, acc_ref = rest
    step = pl.program_id(0)
    qi = qidx_ref[step]
    ki = kidx_ref[step]

    @pl.when(ki == 0)
    def _():
        m_ref[...] = jnp.full(m_ref.shape, NEG, F32)
        l_ref[...] = jnp.zeros(l_ref.shape, F32)
        acc_ref[...] = jnp.zeros(acc_ref.shape, F32)
        for g in range(GROUPS):
            qst_ref[g] = _split_halves(q_ref[:, g * LANES:(g + 1) * LANES])

    if diff:
        rows = lax.broadcasted_iota(I32, (2 * TQ, TK), 0)
        cols = lax.broadcasted_iota(I32, (2 * TQ, TK), 1)
        qpos = qi * TQ + jnp.where(rows >= TQ, rows - TQ, rows)
        keep = (ki * TK + cols) <= qpos
    else:
        b = bias_ref[...].astype(F32)
        bias2 = jnp.concatenate([b, b], axis=0)

    for g in range(GROUPS):
        sl = slice(g * LANES, (g + 1) * LANES)
        s = _dot_t(qst_ref[g], k_ref[:, sl])
        s = jnp.where(keep, s, NEG) if diff else s + bias2
        m_prev = m_ref[g]
        m_new = jnp.maximum(m_prev, jnp.max(s, axis=1, keepdims=True))
        alpha = jnp.exp(m_prev - m_new)
        p = jnp.exp(s - m_new)
        l_ref[g] = alpha * l_ref[g] + jnp.sum(p, axis=1, keepdims=True)
        acc_ref[g] = alpha * acc_ref[g] + jnp.dot(p.astype(BF16), v_ref[:, sl],
                                                  preferred_element_type=F32)
        m_ref[g] = m_new

    @pl.when(ki == (qi * TQ + TQ - 1) // TK)
    def _():
        if diff:
            lq = lq_ref[...]
            lk = lk_ref[...]
            lam = (jnp.exp(jnp.sum(lq[0:1] * lk[0:1], axis=1, keepdims=True))
                   - jnp.exp(jnp.sum(lq[1:2] * lk[1:2], axis=1, keepdims=True)) + lam_init)
        for g in range(GROUPS):
            on = acc_ref[g] / l_ref[g]
            if diff:
                o = on[:TQ] - lam * on[TQ:]
                o_ref[:, g * LANES:(g + 1) * LANES] = _rms(o, sg_ref[...]) * (1.0 - lam_init)
            else:
                lane = lax.broadcasted_iota(I32, (TQ, LANES), 1)
                o_ref[:, g * LANES:(g + 1) * LANES] = jnp.where(lane < HALF, on[:TQ], on[TQ:])


def _causal_pairs(tp):
    qs, ks = [], []
    for qi in range(tp // TQ):
        for ki in range((qi * TQ + TQ - 1) // TK + 1):
            qs.append(qi)
            ks.append(ki)
    return jnp.asarray(np.array(qs, np.int32)), jnp.asarray(np.array(ks, np.int32))


def _flash(pairs, q16, k16, v16, extra, *, diff, lam_init=0.0):
    tp = q16.shape[0]
    qidx, kidx = pairs
    qspec = pl.BlockSpec((TQ, ATT_COLS), lambda s, qi, ki: (qi[s], 0))
    kspec = pl.BlockSpec((TK, ATT_COLS), lambda s, qi, ki: (ki[s], 0))
    if diff:
        extra_specs = [pl.BlockSpec(e.shape, lambda s, qi, ki: (0, 0)) for e in extra]
    else:
        extra_specs = [pl.BlockSpec((TQ, TK), lambda s, qi, ki: (qi[s], ki[s]))]
    return pl.pallas_call(
        functools.partial(_flash_kernel, diff=diff, lam_init=lam_init),
        grid_spec=pltpu.PrefetchScalarGridSpec(
            num_scalar_prefetch=2,
            grid=(qidx.shape[0],),
            in_specs=[qspec, kspec, kspec] + extra_specs,
            out_specs=qspec,
            scratch_shapes=[pltpu.VMEM((GROUPS, 2 * TQ, LANES), BF16),
                            pltpu.VMEM((GROUPS, 2 * TQ, 1), F32),
                            pltpu.VMEM((GROUPS, 2 * TQ, 1), F32),
                            pltpu.VMEM((GROUPS, 2 * TQ, LANES), F32)]),
        out_shape=jax.ShapeDtypeStruct((tp, ATT_COLS), F32),
        compiler_params=_cparams(("arbitrary",)),
        name="diff_attn" if diff else "sparse_attn",
    )(qidx, kidx, q16, k16, v16, *extra)


def _count_bits(n):
    return max(1, int(n - 1).bit_length())


def _index_kernel(qidx_ref, kidx_ref, q_ref, w_ref, kk_ref, bias_ref,
                  key_ref, qst_ref, thr_ref, cst_ref, *, topk, n_valid, tp):
    step = pl.program_id(0)
    qi = qidx_ref[step]
    ki = kidx_ref[step]

    @pl.when(ki == 0)
    def _():
        for g in range(GROUPS):
            qst_ref[g] = _split_halves(q_ref[:, g * LANES:(g + 1) * LANES])

    w = w_ref[...]
    kk = kk_ref[...]
    isc = jnp.zeros((TQ, TK), F32)
    for h in range(IDX_HEADS):
        q_h = qst_ref[h // 2, (h % 2) * TQ:(h % 2 + 1) * TQ, :]
        w_h = w[:, HALF + h:HALF + h + 1] * (IDX_HEADS ** -0.5)
        isc = isc + jnp.maximum(_dot_t(q_h, kk), 0.0) * w_h
    rows = lax.broadcasted_iota(I32, (TQ, TK), 0)
    cols = lax.broadcasted_iota(I32, (TQ, TK), 1)
    causal = (ki * TK + cols) <= (qi * TQ + rows)
    key_ref[:, pl.ds(pl.multiple_of(ki * TK, TK), TK)] = _sort_key(isc, causal)

    @pl.when(ki == (qi * TQ + TQ - 1) // TK)
    def _():
        nchunks = ki + 1

        def count(pred):
            def body(c, cnt):
                tile = key_ref[:, pl.ds(pl.multiple_of(c * TK, TK), TK)]
                col = c * TK + lax.broadcasted_iota(I32, (TQ, TK), 1)
                hit = pred(tile, col).astype(I32)
                for j in range(TK // LANES):
                    cnt = cnt + hit[:, j * LANES:(j + 1) * LANES]
                return cnt
            cnt = lax.fori_loop(0, nchunks, body, jnp.zeros((TQ, LANES), I32))
            return jnp.sum(cnt, axis=1, keepdims=True)

        def bit_body(i, ans):
            cand = ans + (jnp.int32(1) << (31 - i))
            ok = count(lambda t, c: t >= cand) >= topk
            return jnp.where(ok, cand, ans)
        thr = lax.fori_loop(0, 32, bit_body, jnp.full((TQ, 1), INT_MIN, I32))
        n_gt = count(lambda t, c: t > thr)
        n_ge = count(lambda t, c: t >= thr)
        need = topk - n_gt
        row_t = qi * TQ + lax.broadcasted_iota(I32, (TQ, 1), 0)
        excess = (n_ge - n_gt > need) & (thr > INT_MIN) & (row_t < n_valid)
        thr_ref[...] = thr
        cst_ref[...] = jnp.full((TQ, 1), INT_MAX, I32)

        @pl.when(jnp.max(excess.astype(I32)) > 0)
        def _():
            nbits = _count_bits(tp)

            def idx_body(i, pos):
                cand = pos + (jnp.int32(1) << (nbits - 1 - i))
                below = count(lambda t, c: (t == thr) & (c < cand))
                return jnp.where(below < need, cand, pos)
            pos = lax.fori_loop(0, nbits, idx_body, jnp.zeros((TQ, 1), I32))
            cst_ref[...] = jnp.where(excess, pos, INT_MAX)

        thr2 = thr_ref[...]
        cst = cst_ref[...]

        def write(c, _):
            off = pl.multiple_of(c * TK, TK)
            tile = key_ref[:, pl.ds(off, TK)]
            col = c * TK + lax.broadcasted_iota(I32, (TQ, TK), 1)
            sel = ((tile > thr2) | ((tile == thr2) & (col <= cst))) & (tile > INT_MIN)
            bias_ref[:, pl.ds(off, TK)] = jnp.where(sel, 0.0, NEG).astype(BF16)
            return 0
        lax.fori_loop(0, nchunks, write, 0)

        def fill(c, _):
            bias_ref[:, pl.ds(pl.multiple_of(c * TK, TK), TK)] = jnp.full((TQ, TK), NEG, BF16)
            return 0
        lax.fori_loop(nchunks, tp // TK, fill, 0)


def _index(pairs, qi16, kiw, kk16, *, topk, n_valid):
    tp = qi16.shape[0]
    qidx, kidx = pairs
    return pl.pallas_call(
        functools.partial(_index_kernel, topk=topk, n_valid=n_valid, tp=tp),
        grid_spec=pltpu.PrefetchScalarGridSpec(
            num_scalar_prefetch=2,
            grid=(qidx.shape[0],),
            in_specs=[pl.BlockSpec((TQ, ATT_COLS), lambda s, qi, ki: (qi[s], 0)),
                      pl.BlockSpec((TQ, LANES), lambda s, qi, ki: (qi[s], 0)),
                      pl.BlockSpec((TK, LANES), lambda s, qi, ki: (ki[s], 0))],
            out_specs=pl.BlockSpec((TQ, tp), lambda s, qi, ki: (qi[s], 0)),
            scratch_shapes=[pltpu.VMEM((TQ, tp), I32),
                            pltpu.VMEM((GROUPS, 2 * TQ, LANES), BF16),
                            pltpu.VMEM((TQ, 1), I32),
                            pltpu.VMEM((TQ, 1), I32)]),
        out_shape=jax.ShapeDtypeStruct((tp, tp), BF16),
        compiler_params=_cparams(("arbitrary",)),
        name="indexer",
    )(qidx, kidx, qi16, kiw, kk16)


def _mix_kernel(h_ref, oa_ref, ob_ref, gpre_ref, gpost_ref, wg_ref, wa_ref, wb_ref, wo_ref, o_ref):
    h = h_ref[...]
    xn = _rms(h, gpre_ref[...]).astype(BF16)
    gates = 1.0 / (1.0 + jnp.exp(-jnp.dot(xn, wg_ref[...], preferred_element_type=F32)))
    ya = jnp.dot(oa_ref[...].astype(BF16), wa_ref[...], preferred_element_type=F32)
    yb = jnp.dot(ob_ref[...].astype(BF16), wb_ref[...], preferred_element_type=F32)
    mixin = gates[:, :D_MODEL] * ya + gates[:, D_MODEL:] * yb
    mix = jnp.dot(mixin.astype(BF16), wo_ref[...], preferred_element_type=F32)
    o_ref[...] = h + _rms(mix, gpost_ref[...])


def _mix(h, oa, ob, gpre, gpost, wg, wa, wb, wo, tm):
    n = h.shape[0]
    row = lambda w: pl.BlockSpec((tm, w), lambda i: (i, 0))
    full = lambda a: pl.BlockSpec(a.shape, lambda i: (0, 0))
    return pl.pallas_call(
        _mix_kernel,
        grid=(n // tm,),
        in_specs=[row(D_MODEL), row(ATT_COLS), row(ATT_COLS)] + [full(a) for a in (gpre, gpost, wg, wa, wb, wo)],
        out_specs=row(D_MODEL),
        out_shape=jax.ShapeDtypeStruct((n, D_MODEL), F32),
        compiler_params=_cparams(("parallel",)),
        name="mix",
    )(h, oa, ob, gpre, gpost, wg, wa, wb, wo)


def _gelu_tanh(x):
    return x * (0.5 * (1.0 + jnp.tanh(math.sqrt(2.0 / math.pi) * (x + 0.044715 * (x * x * x)))))


def _ffn_kernel(h_ref, pre_ref, gpre_ref, gpost_ref, wup_ref, cw_ref, cb_ref, wdn_ref,
                o_ref, u_ref, carry_ref, *, per_row, tm, tail_blk, tail_row):
    step = pl.program_id(0)
    h = h_ref[...]
    xn = _rms(h, gpre_ref[...]).astype(BF16)
    if not per_row:
        @pl.when(step == 0)
        def _():
            carry_ref[...] = pre_ref[...]
        row = lax.broadcasted_iota(I32, (tm, FF_CHUNK), 0)

    f = jnp.zeros((tm, D_MODEL), F32)
    for c in range(D_FF // FF_CHUNK):
        conv = []
        for half in range(2):
            a = half * D_FF + c * FF_CHUNK
            sl = slice(a, a + FF_CHUNK)
            u = jnp.dot(xn, wup_ref[:, sl], preferred_element_type=F32)
            if per_row:
                u2 = pre_ref[:, sl]
                u1 = pre_ref[:, 2 * D_FF + a: 2 * D_FF + a + FF_CHUNK]
                u_ref[:, sl] = u
            else:
                prev = carry_ref[:, sl]
                p2, p1 = prev[6:7], prev[7:8]
                u1 = jnp.where(row == 0, p1, pltpu.roll(u, 1, axis=0))
                u2 = jnp.where(row == 0, p2, jnp.where(row == 1, p1, pltpu.roll(u, 2, axis=0)))
                carry_ref[:, sl] = u[tm - 8:]

                @pl.when(step == tail_blk)
                def _():
                    u_ref[:, sl] = u[tail_row:tail_row + 8]
            cv = cb_ref[:, sl] + u2 * cw_ref[0:1, sl]
            cv = cv + u1 * cw_ref[1:2, sl]
            cv = cv + u * cw_ref[2:3, sl]
            conv.append(cv)
        act = _gelu_tanh(conv[0]) * conv[1]
        f = f + jnp.dot(act.astype(BF16), wdn_ref[c * FF_CHUNK:(c + 1) * FF_CHUNK, :],
                        preferred_element_type=F32)
    o_ref[...] = h + _rms(f, gpost_ref[...])


def _ffn(h, pre, gpre, gpost, wup, cw, cb, wdn, *, per_row, tm, tail=0):
    n = h.shape[0]
    row = lambda w: pl.BlockSpec((tm, w), lambda i: (i, 0))
    full = lambda a: pl.BlockSpec(a.shape, lambda i: (0, 0))
    if per_row:
        pre_spec, u_spec, u_rows = row(4 * D_FF), row(2 * D_FF), n
        tail_blk = tail_row = 0
    else:
        pre_spec, u_rows = full(pre), 8
        u_spec = pl.BlockSpec((8, 2 * D_FF), lambda i: (0, 0))
        tail_blk, tail_row = tail // tm, (tail % tm) // 8 * 8
    return pl.pallas_call(
        functools.partial(_ffn_kernel, per_row=per_row, tm=tm, tail_blk=tail_blk, tail_row=tail_row),
        grid=(n // tm,),
        in_specs=[row(D_MODEL), pre_spec] + [full(a) for a in (gpre, gpost, wup, cw, cb, wdn)],
        out_specs=[row(D_MODEL), u_spec],
        out_shape=[jax.ShapeDtypeStruct((n, D_MODEL), F32), jax.ShapeDtypeStruct((u_rows, 2 * D_FF), F32)],
        scratch_shapes=[pltpu.VMEM((8, 2 * D_FF), F32)],
        compiler_params=_cparams(("arbitrary",)),
        name="ffn_rows" if per_row else "ffn_seq",
    )(h, pre, gpre, gpost, wup, cw, cb, wdn)


def _decode_kernel(pt_ref, q_ref, kn_ref, vn_ref, *rest, diff, lam_init, n_pg):
    if diff:
        lq_ref, lk_ref, sg_ref = rest[:3]
        rest = rest[3:]
    else:
        bias_ref, bself_ref = rest[:2]
        rest = rest[2:]
    k_refs = rest[:DEC_PAGES]
    v_refs = rest[DEC_PAGES:2 * DEC_PAGES]
    o_ref, qm_ref, m_ref, l_ref, acc_ref = rest[2 * DEC_PAGES:]
    j = pl.program_id(1)
    rows = lax.broadcasted_iota(I32, (8, ATT_COLS), 0)
    lanes = lax.broadcasted_iota(I32, (8, ATT_COLS), 1)

    @pl.when(j == 0)
    def _():
        q = jnp.broadcast_to(q_ref[...].astype(F32), (8, ATT_COLS))
        qm_ref[...] = jnp.where(lanes // HALF == rows, q, 0.0)
        m_ref[...] = jnp.full(m_ref.shape, NEG, F32)
        l_ref[...] = jnp.zeros(l_ref.shape, F32)
        acc_ref[...] = jnp.zeros(acc_ref.shape, F32)

    qm = qm_ref[...]
    s = [_dot_t(qm, k_refs[i][...]) for i in range(DEC_PAGES)]
    if not diff:
        s = [s[i] + bias_ref[i] for i in range(DEC_PAGES)]
    m_prev = m_ref[...]
    m_new = m_prev
    for si in s:
        m_new = jnp.maximum(m_new, jnp.max(si, axis=1, keepdims=True))
    alpha = jnp.exp(m_prev - m_new)
    l_new = alpha * l_ref[...]
    acc = alpha * acc_ref[...]
    for i in range(DEC_PAGES):
        p = jnp.exp(s[i] - m_new)
        l_new = l_new + jnp.sum(p, axis=1, keepdims=True)
        acc = acc + jnp.dot(p, v_refs[i][...], preferred_element_type=F32)
    m_ref[...] = m_new
    l_ref[...] = l_new
    acc_ref[...] = acc

    @pl.when(j == n_pg // DEC_PAGES - 1)
    def _():
        s_self = jnp.sum(qm * kn_ref[...], axis=1, keepdims=True)
        if not diff:
            s_self = s_self + bself_ref[0][:, 0:1]
        m_fin = jnp.maximum(m_new, s_self)
        a_fin = jnp.exp(m_new - m_fin)
        p_self = jnp.exp(s_self - m_fin)
        l_fin = a_fin * l_new + p_self
        on = (a_fin * acc + p_self * vn_ref[...]) / l_fin
        if diff:
            lq = lq_ref[...]
            lk = lk_ref[...]
            lam = (jnp.exp(jnp.sum(lq[0:1] * lk[0:1], axis=1, keepdims=True))
                   - jnp.exp(jnp.sum(lq[1:2] * lk[1:2], axis=1, keepdims=True)) + lam_init)
            mine = lanes // LANES == rows // 2
            o1 = jnp.sum(jnp.where(mine & (rows % 2 == 0), on, 0.0), axis=0, keepdims=True)
            o2 = jnp.sum(jnp.where(mine & (rows % 2 == 1), on, 0.0), axis=0, keepdims=True)
            o = o1 - lam * o2
            for g in range(GROUPS):
                sl = slice(g * LANES, (g + 1) * LANES)
                o_ref[:, sl] = _rms(o[:, sl], sg_ref[...]) * (1.0 - lam_init)
        else:
            o_ref[...] = jnp.sum(jnp.where(lanes // HALF == rows, on, 0.0), axis=0, keepdims=True)


def _decode(page_table, q16, knew, vnew, pool_k, pool_v, extra, *, diff, lam_init=0.0):
    nb, n_pg = page_table.shape
    page = pool_k.shape[1]
    pt = page_table.reshape(-1)
    one = lambda w: pl.BlockSpec((None, 1, w), lambda b, j, pt: (b, 0, 0))

    def page_spec(i):
        return pl.BlockSpec((None, page, ATT_COLS),
                            lambda b, j, pt: (pt[b * n_pg + j * DEC_PAGES + i], 0, 0))
    if diff:
        extra_specs = [pl.BlockSpec(e.shape, lambda b, j, pt: (0, 0)) for e in extra]
        extra_in = list(extra)
    else:
        bias, = extra
        extra_specs = [pl.BlockSpec((None, DEC_PAGES, 1, page), lambda b, j, pt: (b, j, 0, 0)),
                       pl.BlockSpec((None, 8, 1, page), lambda b, j, pt: (b, n_pg // 8, 0, 0))]
        extra_in = [bias, bias]
    return pl.pallas_call(
        functools.partial(_decode_kernel, diff=diff, lam_init=lam_init, n_pg=n_pg),
        grid_spec=pltpu.PrefetchScalarGridSpec(
            num_scalar_prefetch=1,
            grid=(nb, n_pg // DEC_PAGES),
            in_specs=[one(ATT_COLS)] * 3 + extra_specs
                     + [page_spec(i) for i in range(DEC_PAGES)] * 2,
            out_specs=one(ATT_COLS),
            scratch_shapes=[pltpu.VMEM((8, ATT_COLS), F32), pltpu.VMEM((8, 1), F32),
                            pltpu.VMEM((8, 1), F32), pltpu.VMEM((8, ATT_COLS), F32)]),
        out_shape=jax.ShapeDtypeStruct((nb, 1, ATT_COLS), F32),
        compiler_params=_cparams(("parallel", "arbitrary")),
        name="decode_diff" if diff else "decode_sparse",
    )(pt, q16, knew, vnew, *extra_in, *([pool_k] * DEC_PAGES), *([pool_v] * DEC_PAGES))


def _decode_index_kernel(pt_ref, q_ref, w_ref, kn_ref, *rest, topk, n_pg, n_rows):
    k_refs = rest[:IDX_PAGES]
    bias_ref, key_ref = rest[IDX_PAGES:]
    j = pl.program_id(1)
    page = k_refs[0].shape[0]
    q = q_ref[...].astype(F32)
    w = w_ref[...] * (IDX_HEADS ** -0.5)
    for i in range(IDX_PAGES):
        rel = jnp.maximum(_dot_t(q, k_refs[i][...]), 0.0)
        isc = jnp.sum(rel * w, axis=0, keepdims=True)
        key_ref[j * IDX_PAGES + i] = _sort_key(isc, True)

    @pl.when(j == n_pg // IDX_PAGES - 1)
    def _():
        lane = lax.broadcasted_iota(I32, (1, page), 1)
        rel = jnp.maximum(jnp.sum(q * kn_ref[...], axis=1, keepdims=True), 0.0)
        isc = jnp.sum(rel * w, axis=0, keepdims=True)
        key_ref[n_pg] = jnp.where(lane == 0, _sort_key(jnp.broadcast_to(isc, (1, page)), True), INT_MIN)
        for r in range(n_pg + 1, n_rows):
            key_ref[r] = jnp.full((1, page), INT_MIN, I32)

        keys = key_ref[...]
        flat = (lax.broadcasted_iota(I32, keys.shape, 0) * page
                + lax.broadcasted_iota(I32, keys.shape, 2))

        def count(hit):
            c = jnp.sum(hit.astype(I32), axis=0)
            return jnp.sum(c, axis=1, keepdims=True)

        def bit_body(i, ans):
            cand = ans + (jnp.int32(1) << (31 - i))
            return jnp.where(count(keys >= cand) >= topk, cand, ans)
        thr = lax.fori_loop(0, 32, bit_body, jnp.full((1, 1), INT_MIN, I32))
        need = topk - count(keys > thr)
        nbits = _count_bits(n_rows * page)

        def idx_body(i, pos):
            cand = pos + (jnp.int32(1) << (nbits - 1 - i))
            return jnp.where(count((keys == thr) & (flat < cand)) < need, cand, pos)
        pos = lax.fori_loop(0, nbits, idx_body, jnp.zeros((1, 1), I32))
        sel = ((keys > thr) | ((keys == thr) & (flat <= pos))) & (keys > INT_MIN)
        bias_ref[...] = jnp.where(sel, 0.0, NEG)


def _decode_index(page_table, qi, w, kinew, pool_ki, *, topk):
    nb, n_pg = page_table.shape
    page = pool_ki.shape[1]
    n_rows = n_pg + 8
    pt = page_table.reshape(-1)

    def page_spec(i):
        return pl.BlockSpec((None, page, IDX_DIM),
                            lambda b, j, pt: (pt[b * n_pg + j * IDX_PAGES + i], 0, 0))
    return pl.pallas_call(
        functools.partial(_decode_index_kernel, topk=topk, n_pg=n_pg, n_rows=n_rows),
        grid_spec=pltpu.PrefetchScalarGridSpec(
            num_scalar_prefetch=1,
            grid=(nb, n_pg // IDX_PAGES),
            in_specs=[pl.BlockSpec((None, IDX_HEADS, IDX_DIM), lambda b, j, pt: (b, 0, 0)),
                      pl.BlockSpec((None, IDX_HEADS, 1), lambda b, j, pt: (b, 0, 0)),
                      pl.BlockSpec((None, 1, IDX_DIM), lambda b, j, pt: (b, 0, 0))]
                     + [page_spec(i) for i in range(IDX_PAGES)],
            out_specs=pl.BlockSpec((None, n_rows, 1, page), lambda b, j, pt: (b, 0, 0, 0)),
            scratch_shapes=[pltpu.VMEM((n_rows, 1, page), I32)]),
        out_shape=jax.ShapeDtypeStruct((nb, n_rows, 1, page), F32),
        compiler_params=_cparams(("parallel", "arbitrary")),
        name="decode_index",
    )(pt, qi, w, kinew, *([pool_ki] * IDX_PAGES))


def _rope_table(pos):
    rot = HALF // 4
    inv = jnp.float32(ROPE_THETA) ** (-jnp.arange(0, rot, 2, dtype=F32) / rot)
    ang = pos.astype(F32)[:, None] * inv[None, :]
    cos, sin = jnp.cos(ang), jnp.sin(ang)
    n = pos.shape[0]
    one, zero = jnp.ones((n, HALF - rot), F32), jnp.zeros((n, HALF - rot), F32)
    z8 = jnp.zeros((n, rot // 2), F32)
    c64 = jnp.concatenate([cos, cos, one], axis=1)
    s1_64 = jnp.concatenate([-sin, z8, zero], axis=1)
    s2_64 = jnp.concatenate([z8, sin, zero], axis=1)
    i64, z64 = jnp.ones((n, HALF), F32), jnp.zeros((n, HALF), F32)
    return jnp.concatenate([c64, c64, s1_64, s1_64, s2_64, s2_64,
                            c64, i64, s1_64, z64, s2_64, z64], axis=1)


def kernel(x_prompt, x_sample, cache_a_k, cache_a_v, cache_b_k, cache_b_v, cache_b_kidx, state_ffn_conv, page_table, meta_tokens, ln_pre_mix, ln_post_mix, ln_pre_ffn, ln_post_ffn, w_in, lambda_q, lambda_k, subln_gain, w_gates, w_a_out, w_b_out, w_o, w_up, conv_w, conv_b, w_down):
    assert w_in.shape[0] == 1 and x_prompt.shape[0] == 1 and x_sample.shape[1] == 1
    lam_init = 0.8 - 0.6 * math.exp(-0.3 * 0)
    t = x_prompt.shape[1] + N_META
    blk = TQ * TK // math.gcd(TQ, TK)
    tp = -(-t // blk) * blk
    nb, n_pg = page_table.shape
    page = cache_a_k.shape[2]
    past = n_pg * page
    assert t % 8 == 0 and n_pg % IDX_PAGES == 0 and n_pg % 8 == 0

    w16 = jnp.pad(w_in[0], ((0, 0), (0, IN_COLS_PAD - w_in.shape[2]))).astype(BF16)
    wg16, wa16, wb16, wo16 = (a[0].astype(BF16) for a in (w_gates, w_a_out, w_b_out, w_o))
    wup16, wdn16 = w_up[0].astype(BF16), w_down[0].astype(BF16)
    g_pre, g_post, g_pre_f, g_post_f = (a[0][None] for a in (ln_pre_mix, ln_post_mix, ln_pre_ffn, ln_post_ffn))
    sg = subln_gain[0][None]
    lq, lk = lambda_q[0], lambda_k[0]
    cw, cb = conv_w[0], conv_b[0][None]

    h_p = jnp.concatenate([meta_tokens.astype(x_prompt.dtype), x_prompt[0],
                           jnp.zeros((tp - t, D_MODEL), x_prompt.dtype)], axis=0)
    tab_p = _rope_table(jnp.arange(tp, dtype=I32))
    (ka, va, kb, vb, kiw, qa16, ka16, va16, qb16, kb16, vb16, qi16, kk16) = _proj(h_p, g_pre, w16, tab_p, TM)
    pairs = _causal_pairs(tp)
    oa = _flash(pairs, qa16, ka16, va16, (lq, lk, sg), diff=True, lam_init=lam_init)
    bias = _index(pairs, qi16, kiw, kk16, topk=min(TOPK_MAX, t // 4), n_valid=t)
    ob = _flash(pairs, qb16, kb16, vb16, (bias,), diff=False)
    h1 = _mix(h_p, oa, ob, g_pre, g_post, wg16, wa16, wb16, wo16, TM)
    pre_p = jnp.zeros((8, 2 * D_FF), F32)
    h2, u_tail = _ffn(h1, pre_p, g_pre_f, g_post_f, wup16, cw, cb, wdn16, per_row=False, tm=TM, tail=t - 2)
    y_prompt = h2[N_META:t][None]
    ak_p = ka[:t].reshape(1, 1, t, A_HEADS, 2 * A_QK)
    av_p = va[:t].reshape(1, 1, t, A_HEADS, A_V)
    bk_p = kb[:t].reshape(1, 1, t, B_HEADS, B_HD)
    bv_p = vb[:t].reshape(1, 1, t, B_HEADS, B_HD)
    bi_p = kiw[:t, :IDX_DIM].reshape(1, 1, t, IDX_DIM)
    cv_p = u_tail[(t - 2) % 8:(t - 2) % 8 + 2][None, None]

    h_s = x_sample[:, 0]
    tab_s = _rope_table(jnp.full((nb,), past, I32))
    (ka_s, va_s, kb_s, vb_s, kiw_s, qa16_s, _, _, qb16_s, _, _, qi16_s, _) = _proj(h_s, g_pre, w16, tab_s, nb)
    pool = lambda c, w: c[0].reshape(c.shape[1], page, w)
    r3 = lambda a: a[:, None, :]
    oa_s = _decode(page_table, r3(qa16_s), r3(ka_s), r3(va_s), pool(cache_a_k, ATT_COLS), pool(cache_a_v, ATT_COLS),
                   (lq, lk, sg), diff=True, lam_init=lam_init)
    bias_s = _decode_index(page_table, qi16_s.reshape(nb, IDX_HEADS, IDX_DIM),
                           kiw_s[:, HALF:HALF + IDX_HEADS, None], r3(kiw_s[:, :IDX_DIM]),
                           pool(cache_b_kidx, IDX_DIM), topk=min(TOPK_MAX, (past + 1) // 4))
    ob_s = _decode(page_table, r3(qb16_s), r3(kb_s), r3(vb_s), pool(cache_b_k, ATT_COLS), pool(cache_b_v, ATT_COLS),
                   (bias_s,), diff=False)
    h1_s = _mix(h_s, oa_s[:, 0], ob_s[:, 0], g_pre, g_post, wg16, wa16, wb16, wo16, nb)
    st = state_ffn_conv[0]
    h2_s, u_s = _ffn(h1_s, st.reshape(nb, 4 * D_FF), g_pre_f, g_post_f, wup16, cw, cb, wdn16, per_row=True, tm=nb)
    y_sample = h2_s[:, None]
    ak_s = ka_s.reshape(1, nb, 1, A_HEADS, 2 * A_QK)
    av_s = va_s.reshape(1, nb, 1, A_HEADS, A_V)
    bk_s = kb_s.reshape(1, nb, 1, B_HEADS, B_HD)
    bv_s = vb_s.reshape(1, nb, 1, B_HEADS, B_HD)
    bi_s = kiw_s[:, :IDX_DIM].reshape(1, nb, 1, IDX_DIM)
    cv_s = jnp.stack([st[:, 1], u_s], axis=1)[None]
    return (y_prompt, y_sample, ak_p, av_p, bk_p, bv_p, bi_p, cv_p, ak_s, av_s, bk_s, bv_s, bi_s, cv_s)
```
